```python
import jax
import jax.numpy as jnp
from jax import lax
import numpy as np

D_MODEL = 2048
BATCH = 2
SEQ = 8192
DEPTH = 4

GRID_W = 64
CTX_LEN = 256
MIX_W = D_MODEL
FNET_W = MIX_W // 2
FNET_GROUPS = 4
FNET_GROUP_W = FNET_W // FNET_GROUPS
HG_W = MIX_W // 2
HG_DK = 128
HG_DV = 128
HG_HEADS = HG_W // HG_DV
HG_CHUNK = 64
F_MIN = 1e-30
CONV_W = MIX_W // 2
CONV_K = 31
POOL_W = MIX_W // 2
POOL_WINDOWS = (2, 4, 8, 16)
POOL_GROUPS = len(POOL_WINDOWS)
POOL_GROUP_W = POOL_W // POOL_GROUPS
PEER_KEYS = 128
PEER_EXPERTS = PEER_KEYS * PEER_KEYS
PEER_HEADS = 8
PEER_TOPK = 16
PEER_DKEY = 256
PEER_BLOCK = 128

N_EVEN = (DEPTH + 1) // 2
N_ODD = DEPTH // 2
EVEN_IN = FNET_W + 5 * HG_W
ODD_IN = 2 * CONV_W + POOL_W
N_MOD = 6
EPS = 1e-6

kernel_name = 'hybrid_fnet_hgrn2_conformer_pool_peer_dit'


def rms_norm(x, w):
    xf = x.astype(jnp.float32)
    y = xf * lax.rsqrt(jnp.mean(xf * xf, axis=-1, keepdims=True) + EPS)
    return (y * w.astype(jnp.float32)).astype(x.dtype)


def modulate(h, shift, scale):
    return h * (1 + scale) + shift


def sincos_2d(rows, dim):
    quarter = dim // 4
    row = jnp.repeat(jnp.arange(rows), GRID_W).astype(jnp.float32)[:, None]
    col = jnp.tile(jnp.arange(GRID_W), rows).astype(jnp.float32)[:, None]
    omega = 1.0 / (10000.0 ** (jnp.arange(quarter, dtype=jnp.float32) / quarter))
    ar, ac = row * omega, col * omega
    return jnp.concatenate([jnp.sin(ar), jnp.cos(ar), jnp.sin(ac), jnp.cos(ac)], axis=-1)


def rev_segments(a, n_ctx):
    return jnp.concatenate([jnp.flip(a[:, :n_ctx], 1), jnp.flip(a[:, n_ctx:], 1)], axis=1)


def fourier_mix(a):
    B, L, _ = a.shape
    ag = a.astype(jnp.float32).reshape(B, L, FNET_GROUPS, FNET_GROUP_W)
    out = jnp.fft.fft2(ag, axes=(1, 3), norm='ortho').real
    return out.reshape(B, L, FNET_W).astype(a.dtype)


def hgrn2_chunk_scan(q, k, v, log_f, s0):
    N, L, H, _ = q.shape
    nc = L // HG_CHUNK
    to_chunks = lambda t: t.reshape(N, nc, HG_CHUNK, H, t.shape[-1]).transpose(1, 0, 3, 2, 4)
    mask = jnp.tril(jnp.ones((HG_CHUNK, HG_CHUNK), dtype=bool))[:, :, None]

    def step(S, inp):
        qb, kb, vb, gb = inp
        b = jnp.cumsum(gb, axis=2)
        diff = b[:, :, :, None, :] - b[:, :, None, :, :]
        decay = jnp.where(mask, jnp.exp(jnp.where(mask, diff, 0.0)), 0.0)
        A = jnp.einsum('nhtd,nhsd,nhtsd->nhts', qb, kb, decay)
        o = jnp.einsum('nhts,nhsv->nhtv', A, vb) + jnp.einsum('nhtd,nhdv->nhtv', qb * jnp.exp(b), S)
        b_last = b[:, :, -1:, :]
        S_new = jnp.exp(b_last[:, :, 0, :])[..., None] * S + jnp.einsum(
            'nhsd,nhsv->nhdv', kb * jnp.exp(b_last - b), vb)
        return S_new, o

    _, oc = lax.scan(step, s0, (to_chunks(q), to_chunks(k), to_chunks(v), to_chunks(log_f)))
    return oc.transpose(1, 0, 3, 2, 4).reshape(N, L, H, v.shape[-1])


def fourier_hgrn_mixer(hc, hl, w_in, w_out, lb, norm_w):
    B, n_ctx, _ = hc.shape
    h = jnp.concatenate([hc, hl], axis=1)
    L = h.shape[1]
    z = h @ w_in
    a = z[..., :FNET_W]
    q, fl_fwd, fl_bwd, v, g = jnp.split(z[..., FNET_W:], 5, axis=-1)
    four = jnp.concatenate([fourier_mix(a[:, :n_ctx]), fourier_mix(a[:, n_ctx:])], axis=1)
    heads = lambda t: t.astype(jnp.float32).reshape(B, L, HG_HEADS, -1)
    lb_h = lb.reshape(HG_HEADS, HG_DK)

    def gates(fl):
        sig = jax.nn.sigmoid(heads(fl))
        f = lb_h + (1.0 - lb_h) * sig
        log_f = jnp.log(jnp.maximum(f, F_MIN))
        k = (1.0 - lb_h) * (1.0 - sig)
        return log_f, k

    lf_f, k_f = gates(fl_fwd)
    lf_b, k_b = gates(fl_bwd)
    qh = jax.nn.silu(heads(q))
    vh = heads(v)
    both = lambda tf, tb: jnp.concatenate([tf, rev_segments(tb, n_ctx)], axis=0)
    s0 = jnp.zeros((2 * B, HG_HEADS, HG_DK, HG_DV), jnp.float32)
    o2 = hgrn2_chunk_scan(both(qh, qh), both(k_f, k_b), both(vh, vh), both(lf_f, lf_b), s0)
    o = o2[:B] + rev_segments(o2[B:], n_ctx)
    o = o * lax.rsqrt(jnp.mean(o * o, axis=-1, keepdims=True) + EPS) * norm_w.astype(jnp.float32)
    o = (o.reshape(B, L, HG_W) * jax.nn.silu(g.astype(jnp.float32))).astype(h.dtype)
    y = jnp.concatenate([four, o], axis=-1) @ w_out
    return y[:, :n_ctx], y[:, n_ctx:]


def depthwise_conv(u, w):
    return lax.conv_general_dilated(
        u, w.astype(u.dtype)[:, None, :], window_strides=(1,),
        padding=[(CONV_K // 2, CONV_K // 2)],
        dimension_numbers=('NWC', 'WIO', 'NWC'), feature_group_count=u.shape[-1])


def multiscale_pool(p):
    B, L, _ = p.shape
    pg = p.astype(jnp.float32).reshape(B, L, POOL_GROUPS, POOL_GROUP_W)
    csum = jnp.concatenate([jnp.zeros((B, 1, POOL_GROUPS, POOL_GROUP_W), jnp.float32),
                            jnp.cumsum(pg, axis=1)], axis=1)
    t = jnp.arange(L)
    outs = []
    for gi, w in enumerate(POOL_WINDOWS):
        lo = jnp.clip(t - w // 2, 0, L)
        hi = jnp.clip(t + w - w // 2, 0, L)
        cs = csum[:, :, gi]
        mean = (cs[:, hi] - cs[:, lo]) / (hi - lo).astype(jnp.float32)[None, :, None]
        outs.append(mean - pg[:, :, gi])
    return jnp.stack(outs, axis=2)


def conv_pool_mixer(hc, hl, w_in, w_out, dw, ln_w, ln_b, pool_w, pool_scale):
    B, n_ctx, _ = hc.shape
    h = jnp.concatenate([hc, hl], axis=1)
    L = h.shape[1]
    z = h @ w_in
    a, b, p = z[..., :CONV_W], z[..., CONV_W:2 * CONV_W], z[..., 2 * CONV_W:]
    u = a * jax.nn.sigmoid(b)
    u = jnp.concatenate([depthwise_conv(u[:, :n_ctx], dw), depthwise_conv(u[:, n_ctx:], dw)], axis=1)
    u32 = u.astype(jnp.float32)
    mu = jnp.mean(u32, axis=-1, keepdims=True)
    var = jnp.mean(jnp.square(u32 - mu), axis=-1, keepdims=True)
    u = jax.nn.silu((u32 - mu) * lax.rsqrt(var + EPS) * ln_w.astype(jnp.float32)
                    + ln_b.astype(jnp.float32)).astype(h.dtype)
    pooled = jnp.concatenate([multiscale_pool(p[:, :n_ctx]), multiscale_pool(p[:, n_ctx:])], axis=1)
    pm = jnp.einsum('blgc,gcd->blgd', pooled, pool_w.astype(jnp.float32)).reshape(B, L, POOL_W)
    pm = (pm * pool_scale.astype(jnp.float32)).astype(h.dtype)
    y = jnp.concatenate([u, pm], axis=-1) @ w_out
    return y[:, :n_ctx], y[:, n_ctx:]


def peer_ffn(h, q_w, keys, u, v):
    T, D = h.shape
    q = (h @ q_w).astype(jnp.float32).reshape(T, PEER_HEADS, 2, PEER_DKEY // 2)
    s = jnp.einsum('thpd,hpkd->thpk', q, keys.astype(jnp.float32))
    top_s, top_i = lax.top_k(s, PEER_TOPK)
    cand_s = top_s[:, :, 0, :, None] + top_s[:, :, 1, None, :]
    cand_i = top_i[:, :, 0, :, None] * PEER_KEYS + top_i[:, :, 1, None, :]
    best_s, best_j = lax.top_k(cand_s.reshape(T, PEER_HEADS, PEER_TOPK * PEER_TOPK), PEER_TOPK)
    idx = jnp.take_along_axis(cand_i.reshape(T, PEER_HEADS, PEER_TOPK * PEER_TOPK), best_j, axis=-1)
    gate = jax.nn.softmax(best_s, axis=-1)
    hk = PEER_HEADS * PEER_TOPK
    nb = T // PEER_BLOCK

    def block(args):
        hb, ib, gb = args
        act = jax.nn.gelu(jnp.einsum('td,ted->te', hb, u[ib]), approximate=False)
        return jnp.einsum('te,ted->td', gb * act, v[ib])

    out = lax.map(block, (h.reshape(nb, PEER_BLOCK, D),
                          idx.reshape(nb, PEER_BLOCK, hk),
                          gate.reshape(nb, PEER_BLOCK, hk).astype(h.dtype)))
    return out.reshape(T, D)


def setup_inputs(seed: int = 0) -> dict:
    key = jax.random.key(seed)
    ks = jax.random.split(key, 24)
    D = D_MODEL

    def nrm(k, shape, std):
        return std * jax.random.normal(k, shape, jnp.float32)

    return {
        'x': nrm(ks[0], (BATCH, SEQ, D), 1.0),
        'c': nrm(ks[1], (BATCH, D), 1.0),
        'ctx': nrm(ks[2], (BATCH, CTX_LEN, D), 1.0),
        'c_ctx': nrm(ks[3], (D,), 1.0),
        'ada_w': nrm(ks[4], (DEPTH, D, N_MOD * D), 0.5 * D ** -0.5),
        'ada_b': nrm(ks[5], (DEPTH, N_MOD * D), 0.02),
        'norm1_w': 1.0 + nrm(ks[6], (DEPTH, D), 0.05),
        'norm2_w': 1.0 + nrm(ks[7], (DEPTH, D), 0.05),
        'final_norm_w': 1.0 + nrm(ks[8], (D,), 0.05),
        'ev_in_w': nrm(ks[9], (N_EVEN, D, EVEN_IN), D ** -0.5),
        'ev_out_w': nrm(ks[10], (N_EVEN, MIX_W, D), MIX_W ** -0.5),
        'hg_lb_logits': nrm(ks[11], (N_EVEN, HG_W), 0.5),
        'hg_norm_w': 1.0 + nrm(ks[12], (N_EVEN, HG_DV), 0.05),
        'od_in_w': nrm(ks[13], (N_ODD, D, ODD_IN), D ** -0.5),
        'od_out_w': nrm(ks[14], (N_ODD, MIX_W, D), MIX_W ** -0.5),
        'conv_dw': nrm(ks[15], (N_ODD, CONV_K, CONV_W), CONV_K ** -0.5),
        'conv_ln_w': 1.0 + nrm(ks[16], (N_ODD, CONV_W), 0.05),
        'conv_ln_b': nrm(ks[17], (N_ODD, CONV_W), 0.02),
        'pool_w': nrm(ks[18], (N_ODD, POOL_GROUPS, POOL_GROUP_W, POOL_GROUP_W), POOL_GROUP_W ** -0.5),
        'pool_scale': 1.0 + nrm(ks[19], (N_ODD, POOL_W), 0.1),
        'peer_q_w': nrm(ks[20], (DEPTH, D, PEER_HEADS * PEER_DKEY), D ** -0.5),
        'peer_keys': nrm(ks[21], (DEPTH, PEER_HEADS, 2, PEER_KEYS, PEER_DKEY // 2), (PEER_DKEY // 2) ** -0.5),
        'peer_u': nrm(ks[22], (DEPTH, PEER_EXPERTS, D), D ** -0.5),
        'peer_v': nrm(ks[23], (DEPTH, PEER_EXPERTS, D), 0.5),
    }


def reference(x, c, ctx, c_ctx, ada_w, ada_b, norm1_w, norm2_w, final_norm_w,
              ev_in_w, ev_out_w, hg_lb_logits, hg_norm_w,
              od_in_w, od_out_w, conv_dw, conv_ln_w, conv_ln_b, pool_w, pool_scale,
              peer_q_w, peer_keys, peer_u, peer_v):
    B, S, D = x.shape
    n_ctx = ctx.shape[1]
    rows = S // GRID_W
    xl = x + sincos_2d(rows, D).astype(x.dtype)[None]
    xc = ctx
    lb_p = jax.nn.softmax(hg_lb_logits.astype(jnp.float32), axis=0)
    lower_bounds = jnp.concatenate([jnp.zeros_like(lb_p[:1]), jnp.cumsum(lb_p[1:], axis=0)], axis=0)
    lower_bounds = jnp.clip(lower_bounds, 0.0, 1.0)
    sc_l = jax.nn.silu(c)
    sc_c = jax.nn.silu(c_ctx)
    for l in range(DEPTH):
        mod_l = (sc_l @ ada_w[l] + ada_b[l]).reshape(B, 1, N_MOD, D)
        mod_c = (sc_c @ ada_w[l] + ada_b[l]).reshape(N_MOD, D)
        hc = modulate(rms_norm(xc, norm1_w[l]), mod_c[0], mod_c[1])
        hl = modulate(rms_norm(xl, norm1_w[l]), mod_l[:, :, 0], mod_l[:, :, 1])
        j = l // 2
        if l % 2 == 0:
            yc, yl = fourier_hgrn_mixer(hc, hl, ev_in_w[j], ev_out_w[j], lower_bounds[j], hg_norm_w[j])
        else:
            yc, yl = conv_pool_mixer(hc, hl, od_in_w[j], od_out_w[j], conv_dw[j], conv_ln_w[j],
                                     conv_ln_b[j], pool_w[j], pool_scale[j])
        xc = xc + mod_c[2] * yc
        xl = xl + mod_l[:, :, 2] * yl
        hc = modulate(rms_norm(xc, norm2_w[l]), mod_c[3], mod_c[4])
        hl = modulate(rms_norm(xl, norm2_w[l]), mod_l[:, :, 3], mod_l[:, :, 4])
        h_all = jnp.concatenate([hc.reshape(B * n_ctx, D), hl.reshape(B * S, D)], axis=0)
        y_all = peer_ffn(h_all, peer_q_w[l], peer_keys[l], peer_u[l], peer_v[l])
        xc = xc + mod_c[5] * y_all[:B * n_ctx].reshape(B, n_ctx, D)
        xl = xl + mod_l[:, :, 5] * y_all[B * n_ctx:].reshape(B, S, D)
    return rms_norm(xl, final_norm_w)
```

```python
import functools
import math

import numpy as np
import jax
import jax.numpy as jnp
from jax import lax
from jax.experimental import pallas as pl
from jax.experimental.pallas import tpu as pltpu

F32 = jnp.float32
BF16 = jnp.bfloat16

GRID_W = 64
FNET_GROUPS = 4
HG_HEAD_W = 128
F_MIN = 1e-30
CONV_K = 31
POOL_WINDOWS = (2, 4, 8, 16)
PEER_KEYS = 128
PEER_HEADS = 8
PEER_TOPK = 16
N_MOD = 6
EPS = 1e-6

LANES = 128
SUBLANES = 8
VMEM_LIMIT_BYTES = 56 * 1024 * 1024

INV_SQRT2 = 1.0 / math.sqrt(2.0)


def _cparams(sem):
    return pltpu.CompilerParams(dimension_semantics=sem, vmem_limit_bytes=VMEM_LIMIT_BYTES)


def _dot(a, b):
    return jnp.dot(a, b, preferred_element_type=F32)


def _dot_nt(a, b):
    return lax.dot_general(a, b, (((1,), (1,)), ((), ())), preferred_element_type=F32)


def _dot_tn(a, b):
    return lax.dot_general(a, b, (((0,), (0,)), ((), ())), preferred_element_type=F32)


def _silu(x):
    return x * jax.nn.sigmoid(x)


def _embed_kernel(x_ref, pos_ref, ctx_ref, o_ref, *, n_lat):
    i = pl.program_id(1)

    @pl.when(i < n_lat)
    def _():
        o_ref[...] = x_ref[...] + pos_ref[...]

    @pl.when(i >= n_lat)
    def _():
        o_ref[...] = ctx_ref[...]


def _embed(x, pos, ctx, tr):
    B, S, D = x.shape
    nc = ctx.shape[1]
    n_lat, n_ctx = S // tr, nc // tr
    return pl.pallas_call(
        functools.partial(_embed_kernel, n_lat=n_lat),
        grid=(B, n_lat + n_ctx),
        in_specs=[
            pl.BlockSpec((None, tr, D), lambda b, i: (b, jnp.minimum(i, n_lat - 1), 0)),
            pl.BlockSpec((tr, D), lambda b, i: (jnp.minimum(i, n_lat - 1), 0)),
            pl.BlockSpec((None, tr, D), lambda b, i: (b, jnp.maximum(i - n_lat, 0), 0)),
        ],
        out_specs=pl.BlockSpec((None, tr, D), lambda b, i: (b, i, 0)),
        out_shape=jax.ShapeDtypeStruct((B, S + nc, D), F32),
        compiler_params=_cparams(("parallel", "arbitrary")),
        name="embed",
    )(x, pos, ctx)


def _sincos_table(rows, dim):
    quarter = dim // 4
    row = jnp.repeat(jnp.arange(rows), GRID_W).astype(F32)[:, None]
    col = jnp.tile(jnp.arange(GRID_W), rows).astype(F32)[:, None]
    omega = 1.0 / (10000.0 ** (jnp.arange(quarter, dtype=F32) / quarter))
    ar, ac = row * omega, col * omega
    return jnp.concatenate([jnp.sin(ar), jnp.cos(ar), jnp.sin(ac), jnp.cos(ac)], axis=-1)


def _mod_kernel(c_ref, w_ref, b_ref, o_ref):
    sc = _silu(c_ref[...]).astype(BF16)
    o_ref[...] = _dot(sc, w_ref[...].astype(BF16)) + b_ref[...]


def _modulation(cvec, ada_w, ada_b, tn):
    depth, D, n = ada_w.shape
    rows = cvec.shape[0]
    return pl.pallas_call(
        _mod_kernel,
        grid=(depth, n // tn),
        in_specs=[
            pl.BlockSpec((rows, D), lambda l, j: (0, 0)),
            pl.BlockSpec((None, D, tn), lambda l, j: (l, 0, j)),
            pl.BlockSpec((None, 1, tn), lambda l, j: (l, 0, j)),
        ],
        out_specs=pl.BlockSpec((None, rows, tn), lambda l, j: (l, 0, j)),
        out_shape=jax.ShapeDtypeStruct((depth, rows, n), F32),
        compiler_params=_cparams(("parallel", "arbitrary")),
        name="modulation",
    )(cvec, ada_w, ada_b.reshape(depth, 1, n))


def _mod_rows(mod_ref, b, k, is_ctx, ctx_row):
    lat = mod_ref[b, pl.ds(k, 1), :]
    cx = mod_ref[ctx_row, pl.ds(k, 1), :]
    return jnp.where(is_ctx, cx, lat)


def _norm_mod(x, nw, shift, scale):
    ms = jnp.mean(x * x, axis=-1, keepdims=True)
    y = x * lax.rsqrt(ms + EPS) * nw
    return y * (1.0 + scale) + shift


def _is_ctx_col(i, tm, n_lat_rows):
    rows = i * tm + lax.broadcasted_iota(jnp.int32, (tm, 1), 0)
    return rows >= n_lat_rows


def _norm_matmul_kernel(x_ref, nw_ref, mod_ref, w_ref, o_ref, h_ref, *, tm, n_lat_rows, ks, ctx_row):
    b, i, j = pl.program_id(0), pl.program_id(1), pl.program_id(2)

    @pl.when(j == 0)
    def _():
        is_ctx = _is_ctx_col(i, tm, n_lat_rows)
        shift = _mod_rows(mod_ref, b, ks[0], is_ctx, ctx_row)
        scale = _mod_rows(mod_ref, b, ks[1], is_ctx, ctx_row)
        h_ref[...] = _norm_mod(x_ref[...], nw_ref[...], shift, scale).astype(BF16)

    o_ref[...] = _dot(h_ref[...], w_ref[...]).astype(o_ref.dtype)


def _norm_matmul(x, nw, mod, w, *, ks, n_lat_rows, tm, tn, out_dtype=F32):
    B, L, D = x.shape
    n = w.shape[1]
    kern = functools.partial(_norm_matmul_kernel, tm=tm, n_lat_rows=n_lat_rows, ks=ks, ctx_row=B)
    return pl.pallas_call(
        kern,
        grid=(B, L // tm, n // tn),
        in_specs=[
            pl.BlockSpec((None, tm, D), lambda b, i, j: (b, i, 0)),
            pl.BlockSpec((1, D), lambda b, i, j: (0, 0)),
            pl.BlockSpec(mod.shape, lambda b, i, j: (0, 0, 0)),
            pl.BlockSpec((D, tn), lambda b, i, j: (0, j)),
        ],
        out_specs=pl.BlockSpec((None, tm, tn), lambda b, i, j: (b, i, j)),
        out_shape=jax.ShapeDtypeStruct((B, L, n), out_dtype),
        scratch_shapes=[pltpu.VMEM((tm, D), BF16)],
        compiler_params=_cparams(("parallel", "parallel", "arbitrary")),
        name="norm_matmul",
    )(x, nw.reshape(1, D), mod, w)


def _out_proj_kernel(m_ref, w_ref, x_ref, gate_ref, o_ref, *, tm, n_lat_rows, ctx_row):
    b, i = pl.program_id(0), pl.program_id(1)
    is_ctx = _is_ctx_col(i, tm, n_lat_rows)
    gate = jnp.where(is_ctx, gate_ref[ctx_row], gate_ref[b])
    o_ref[...] = x_ref[...] + gate * _dot(m_ref[...], w_ref[...])


def _out_proj(m, w, x, mod_gate, *, n_lat_rows, tm, tn):
    B, L, D = x.shape
    K = m.shape[2]
    kern = functools.partial(_out_proj_kernel, tm=tm, n_lat_rows=n_lat_rows, ctx_row=B)
    return pl.pallas_call(
        kern,
        grid=(B, L // tm, D // tn),
        in_specs=[
            pl.BlockSpec((None, tm, K), lambda b, i, j: (b, i, 0)),
            pl.BlockSpec((K, tn), lambda b, i, j: (0, j)),
            pl.BlockSpec((None, tm, tn), lambda b, i, j: (b, i, j)),
            pl.BlockSpec((mod_gate.shape[0], 1, tn), lambda b, i, j: (0, 0, j)),
        ],
        out_specs=pl.BlockSpec((None, tm, tn), lambda b, i, j: (b, i, j)),
        out_shape=jax.ShapeDtypeStruct((B, L, D), F32),
        compiler_params=_cparams(("parallel", "parallel", "arbitrary")),
        name="out_proj",
    )(m, w, x, mod_gate)


def _dft_mats(n):
    k = np.arange(n)
    ang = 2.0 * np.pi * ((k[:, None] * k[None, :]) % n) / n
    return np.cos(ang), np.sin(ang)


def _fa_kernel(x_ref, c_ref, s_ref, tc_ref, ts_ref, vr_ref, vi_ref, *, n_sub, cw):
    cm, sm = c_ref[...], s_ref[...]
    for s in range(n_sub):
        sl = slice(s * cw, (s + 1) * cw)
        xs = x_ref[:, sl].astype(BF16)
        a = _dot(cm, xs)
        bneg = _dot(sm, xs)
        tc = tc_ref[:, s:s + 1]
        ts = ts_ref[:, s:s + 1]
        vr_ref[:, sl] = (a * tc - bneg * ts).astype(BF16)
        vi_ref[:, sl] = (-bneg * tc - a * ts).astype(BF16)


def _fb_kernel(vr_ref, vi_ref, c_ref, s_ref, cc_ref, sc_ref, o_ref, *, groups, gw, scale):
    cm, sm = c_ref[...], s_ref[...]
    vr, vi = vr_ref[...], vi_ref[...]
    ur = (_dot(cm, vr) + _dot(sm, vi)).astype(BF16)
    ui = (_dot(cm, vi) - _dot(sm, vr)).astype(BF16)
    cc, sc = cc_ref[...], sc_ref[...]
    for g in range(groups):
        sl = slice(g * gw, (g + 1) * gw)
        o = _dot(ur[:, sl], cc) + _dot(ui[:, sl], sc)
        o_ref[:, sl] = (o * scale).astype(o_ref.dtype)


def _fctx_kernel(x_ref, c_ref, s_ref, cc_ref, sc_ref, o_ref, *, groups, gw, scale):
    x = x_ref[...].astype(BF16)
    ur = _dot(c_ref[...], x).astype(BF16)
    uineg = _dot(s_ref[...], x).astype(BF16)
    cc, sc = cc_ref[...], sc_ref[...]
    for g in range(groups):
        sl = slice(g * gw, (g + 1) * gw)
        o = _dot(ur[:, sl], cc) - _dot(uineg[:, sl], sc)
        o_ref[:, sl] = (o * scale).astype(o_ref.dtype)


def _fourier_mix(a, n_lat, n_ctx):
    B, L, W = a.shape
    gw = W // FNET_GROUPS
    n1, n2 = 128, n_lat // 128
    assert n1 * n2 == n_lat and L == n_lat + n_ctx and n_lat % n_ctx == 0
    c2, s2 = _dft_mats(n2)
    c1, s1 = _dft_mats(n1)
    cc, sc = _dft_mats(gw)
    cx, sx = _dft_mats(n_ctx)
    n_sub = 8
    tw_ang = 2.0 * np.pi * ((np.arange(n2)[:, None] * np.arange(n1)[None, :]) % n_lat) / n_lat
    tc = np.cos(tw_ang).reshape(n2, n1 // n_sub, n_sub).transpose(1, 0, 2)
    ts = np.sin(tw_ang).reshape(n2, n1 // n_sub, n_sub).transpose(1, 0, 2)
    bf = lambda m: jnp.asarray(m, dtype=BF16)
    f32 = lambda m: jnp.asarray(m, dtype=F32)

    blk = n_sub * W
    a_v = a.reshape(B, L // n1, n1 * W)
    full2 = lambda shape: pl.BlockSpec(shape, lambda b, j: (0, 0))
    vr, vi = pl.pallas_call(
        functools.partial(_fa_kernel, n_sub=n_sub, cw=W),
        grid=(B, n1 // n_sub),
        in_specs=[
            pl.BlockSpec((None, n2, blk), lambda b, j: (b, 0, j)),
            full2((n2, n2)), full2((n2, n2)),
            pl.BlockSpec((None, n2, n_sub), lambda b, j: (j, 0, 0)),
            pl.BlockSpec((None, n2, n_sub), lambda b, j: (j, 0, 0)),
        ],
        out_specs=[pl.BlockSpec((None, n2, blk), lambda b, j: (b, 0, j))] * 2,
        out_shape=[jax.ShapeDtypeStruct((B, n2, n1 * W), BF16)] * 2,
        compiler_params=_cparams(("parallel", "arbitrary")),
        name="fourier_a",
    )(a_v, bf(c2), bf(s2), f32(tc), f32(ts))

    vr = vr.reshape(B, n_lat, W)
    vi = vi.reshape(B, n_lat, W)
    lat = pl.pallas_call(
        functools.partial(_fb_kernel, groups=FNET_GROUPS, gw=gw, scale=1.0 / math.sqrt(n_lat * gw)),
        grid=(B, n2),
        in_specs=[
            pl.BlockSpec((None, n1, W), lambda b, k: (b, k, 0)),
            pl.BlockSpec((None, n1, W), lambda b, k: (b, k, 0)),
            full2((n1, n1)), full2((n1, n1)), full2((gw, gw)), full2((gw, gw)),
        ],
        out_specs=pl.BlockSpec((None, n1, W), lambda b, k: (b, 0, k)),
        out_shape=jax.ShapeDtypeStruct((B, n1, n2 * W), BF16),
        compiler_params=_cparams(("parallel", "arbitrary")),
        name="fourier_b",
    )(vr, vi, bf(c1), bf(s1), bf(cc), bf(sc))
    lat = lat.reshape(B, n_lat, W)

    full1 = lambda shape: pl.BlockSpec(shape, lambda b: (0, 0))
    cxo = pl.pallas_call(
        functools.partial(_fctx_kernel, groups=FNET_GROUPS, gw=gw, scale=1.0 / math.sqrt(n_ctx * gw)),
        grid=(B,),
        in_specs=[
            pl.BlockSpec((None, n_ctx, W), lambda b: (b, n_lat // n_ctx, 0)),
            full1((n_ctx, n_ctx)), full1((n_ctx, n_ctx)), full1((gw, gw)), full1((gw, gw)),
        ],
        out_specs=pl.BlockSpec((None, n_ctx, W), lambda b: (b, 0, 0)),
        out_shape=jax.ShapeDtypeStruct((B, n_ctx, W), BF16),
        compiler_params=_cparams(("parallel",)),
        name="fourier_ctx",
    )(a, bf(cx), bf(sx), bf(cc), bf(sc))
    return jnp.concatenate([lat, cxo], axis=1)


def _hgrn_kernel(q_ref, f_ref, v_ref, lb_ref, o_ref, st_ref, *, reverse, n_units, n_heads):
    c = pl.program_id(1)

    @pl.when(c == 0)
    def _():
        st_ref[...] = jnp.zeros_like(st_ref)

    U = SUBLANES
    hw = HG_HEAD_W
    row = lax.broadcasted_iota(jnp.int32, (U, hw), 0)
    rowc = lax.broadcasted_iota(jnp.int32, (U, 1), 0)

    def unit(u, carry):
        uu = (n_units - 1 - u) if reverse else u
        r0 = pl.multiple_of(uu * U, U)
        for h in range(n_heads):
            sl = slice(h * hw, (h + 1) * hw)
            q8 = q_ref[pl.ds(r0, U), sl]
            f8 = f_ref[pl.ds(r0, U), sl]
            v8 = v_ref[pl.ds(r0, U), sl]
            lb = lb_ref[:, sl]
            sig = jax.nn.sigmoid(f8)
            fe = jnp.maximum(lb + (1.0 - lb) * sig, F_MIN)
            kk = (1.0 - lb) * (1.0 - sig)
            qs = _silu(q8)
            pre, suf = fe, fe
            for s in (1, 2, 4):
                pre = pre * jnp.where(row >= s, pltpu.roll(pre, s, 0), 1.0)
                suf = suf * jnp.where(row <= U - 1 - s, pltpu.roll(suf, U - s, 0), 1.0)
            if not reverse:
                qp = qs * pre
                kp = kk * jnp.where(row <= U - 2, pltpu.roll(suf, U - 1, 0), 1.0)
                dec = pre[U - 1:U, :]
            else:
                qp = qs * suf
                kp = kk * jnp.where(row >= 1, pltpu.roll(pre, 1, 0), 1.0)
                dec = suf[0:1, :]
            o = jnp.sum(qs * kk, axis=-1, keepdims=True) * v8
            d = None
            for lag in range(1, U):
                sh = lag if not reverse else U - lag
                shf = (lag - 1) if not reverse else (U - (lag - 1)) % U
                fsh = fe if shf == 0 else pltpu.roll(fe, shf, 0)
                d = fsh if d is None else d * fsh
                kd = pltpu.roll(kk, sh, 0)
                vd = pltpu.roll(v8, sh, 0)
                rs = jnp.sum(qs * kd * d, axis=-1, keepdims=True)
                ok = (rowc >= lag) if not reverse else (rowc <= U - 1 - lag)
                o = o + jnp.where(ok, rs, 0.0) * vd
            st = st_ref[h]
            o = o + _dot_nt(qp.astype(BF16), st.astype(BF16))
            o_ref[pl.ds(r0, U), sl] = o
            st_ref[h] = st * dec + _dot_tn(v8.astype(BF16), kp.astype(BF16))
        return carry

    lax.fori_loop(0, n_units, unit, 0)


def _hgrn(z, lb, *, reverse, n_lat, n_ctx, tc, col0):
    B, L, _ = z.shape
    W = lb.shape[-1]
    n_heads = W // HG_HEAD_W
    n_lat_c, n_ctx_c = n_lat // tc, n_ctx // tc
    n_chunks = n_lat_c + n_ctx_c

    def chunk(c):
        if not reverse:
            return jnp.where(c < n_ctx_c, n_lat_c + c, c - n_ctx_c)
        return jnp.where(c < n_ctx_c, n_chunks - 1 - c, n_chunks - 1 - c)

    fcol = col0 + (2 if reverse else 1)
    kern = functools.partial(_hgrn_kernel, reverse=reverse, n_units=tc // SUBLANES, n_heads=n_heads)
    return pl.pallas_call(
        kern,
        grid=(B, n_chunks),
        in_specs=[
            pl.BlockSpec((None, tc, W), lambda b, c: (b, chunk(c), col0)),
            pl.BlockSpec((None, tc, W), lambda b, c: (b, chunk(c), fcol)),
            pl.BlockSpec((None, tc, W), lambda b, c: (b, chunk(c), col0 + 3)),
            pl.BlockSpec((1, W), lambda b, c: (0, 0)),
        ],
        out_specs=pl.BlockSpec((None, tc, W), lambda b, c: (b, chunk(c), 0)),
        out_shape=jax.ShapeDtypeStruct((B, L, W), F32),
        scratch_shapes=[pltpu.VMEM((n_heads, HG_HEAD_W, HG_HEAD_W), F32)],
        compiler_params=_cparams(("parallel", "arbitrary")),
        name="hgrn_bwd" if reverse else "hgrn_fwd",
    )(z, z, z, lb.reshape(1, W))


def _hg_combine_kernel(of_ref, ob_ref, g_ref, four_ref, nw_ref, m_ref, *, n_heads, W):
    hw = HG_HEAD_W
    nw = nw_ref[...]
    m_ref[:, :W] = four_ref[...]
    for h in range(n_heads):
        sl = slice(h * hw, (h + 1) * hw)
        o = of_ref[:, sl] + ob_ref[:, sl]
        ms = jnp.mean(o * o, axis=-1, keepdims=True)
        y = o * lax.rsqrt(ms + EPS) * nw * _silu(g_ref[:, sl])
        m_ref[:, W + h * hw:W + (h + 1) * hw] = y.astype(BF16)


def _hg_combine(of, ob, z, four, nw, *, gcol, tm):
    B, L, W = of.shape
    kern = functools.partial(_hg_combine_kernel, n_heads=W // HG_HEAD_W, W=W)
    return pl.pallas_call(
        kern,
        grid=(B, L // tm),
        in_specs=[
            pl.BlockSpec((None, tm, W), lambda b, i: (b, i, 0)),
            pl.BlockSpec((None, tm, W), lambda b, i: (b, i, 0)),
            pl.BlockSpec((None, tm, W), lambda b, i: (b, i, gcol)),
            pl.BlockSpec((None, tm, W), lambda b, i: (b, i, 0)),
            pl.BlockSpec((1, HG_HEAD_W), lambda b, i: (0, 0)),
        ],
        out_specs=pl.BlockSpec((None, tm, 2 * W), lambda b, i: (b, i, 0)),
        out_shape=jax.ShapeDtypeStruct((B, L, 2 * W), BF16),
        compiler_params=_cparams(("parallel", "arbitrary")),
        name="hg_combine",
    )(of, ob, z, four, nw.reshape(1, HG_HEAD_W))


HALO = 16


def _conv_pool_kernel(a_ref, b_ref, p_ref, ap_ref, bp_ref, pp_ref, an_ref, bn_ref, pn_ref,
                      dw_ref, lnw_ref, lnb_ref, pw_ref, ps_ref, o_ref, ue_ref, pe_ref, cv_ref,
                      *, tr, W, n_lat_t, n_lat_rows, n_ctx_rows):
    i = pl.program_id(1)
    is_ctx_t = i >= n_lat_t
    first = jnp.logical_or(i == 0, i == n_lat_t)
    last = jnp.logical_or(i == n_lat_t - 1, is_ctx_t)
    pv = jnp.where(first, 0.0, 1.0)
    nv = jnp.where(last, 0.0, 1.0)

    ext = tr + 2 * HALO
    ue_ref[0, 0:HALO, :] = ap_ref[...] * jax.nn.sigmoid(bp_ref[...]) * pv
    ue_ref[0, HALO:HALO + tr, :] = a_ref[...] * jax.nn.sigmoid(b_ref[...])
    ue_ref[0, HALO + tr:ext, :] = an_ref[...] * jax.nn.sigmoid(bn_ref[...]) * nv
    pe_ref[0, 0:HALO, :] = pp_ref[...] * pv
    pe_ref[0, HALO:HALO + tr, :] = p_ref[...]
    pe_ref[0, HALO + tr:ext, :] = pn_ref[...] * nv
    CP = 2 * HALO
    for r in range(1, SUBLANES):
        for c0 in range(0, ext - SUBLANES, CP):
            n = min(CP, ext - SUBLANES - c0)
            ue_ref[r, c0:c0 + n, :] = ue_ref[0, c0 + r:c0 + r + n, :]
            pe_ref[r, c0:c0 + n, :] = pe_ref[0, c0 + r:c0 + r + n, :]

    RC = 32
    half = CONV_K // 2

    def shifted(ref, r0, off, ls):
        return ref[off % SUBLANES, pl.ds(pl.multiple_of(r0 + (off - off % SUBLANES), SUBLANES), RC), ls]

    def conv_rows(rc, carry):
        r0 = pl.multiple_of(rc * RC, RC)
        for lc in range(W // LANES):
            ls = slice(lc * LANES, (lc + 1) * LANES)
            acc = jnp.zeros((RC, LANES), F32)
            for j in range(CONV_K):
                acc = acc + shifted(ue_ref, r0, HALO - half + j, ls) * dw_ref[j:j + 1, ls]
            cv_ref[pl.ds(r0, RC), ls] = acc
        return carry

    lax.fori_loop(0, tr // RC, conv_rows, 0)

    seg_len = jnp.where(is_ctx_t, n_ctx_rows, n_lat_rows)
    t_base = jnp.where(is_ctx_t, 0, i * tr)
    gw = W // len(POOL_WINDOWS)

    def norm_pool_rows(rc, carry):
        r0 = pl.multiple_of(rc * RC, RC)
        u = cv_ref[pl.ds(r0, RC), :]
        mu = jnp.mean(u, axis=-1, keepdims=True)
        xc = u - mu
        var = jnp.mean(xc * xc, axis=-1, keepdims=True)
        y = _silu(xc * lax.rsqrt(var + EPS) * lnw_ref[...] + lnb_ref[...])
        o_ref[pl.ds(r0, RC), 0:W] = y.astype(BF16)
        t_loc = t_base + r0 + lax.broadcasted_iota(jnp.int32, (RC, 1), 0)
        for g, win in enumerate(POOL_WINDOWS):
            ls = slice(g * gw, (g + 1) * gw)
            acc = jnp.zeros((RC, gw), F32)
            for k in range(win):
                acc = acc + shifted(pe_ref, r0, HALO - win // 2 + k, ls)
            lo = jnp.maximum(t_loc - win // 2, 0)
            hi = jnp.minimum(t_loc + (win - win // 2), seg_len)
            cnt = (hi - lo).astype(F32)
            cv_ref[pl.ds(r0, RC), ls] = acc / cnt - shifted(pe_ref, r0, HALO, ls)
        return carry

    lax.fori_loop(0, tr // RC, norm_pool_rows, 0)

    for g in range(len(POOL_WINDOWS)):
        ls = slice(g * gw, (g + 1) * gw)
        pm = _dot(cv_ref[:, ls].astype(BF16), pw_ref[g]) * ps_ref[:, ls]
        o_ref[:, W + g * gw:W + (g + 1) * gw] = pm.astype(BF16)


def _conv_pool(z, dw, lnw, lnb, pw, ps, *, n_lat, n_ctx, tr):
    B, L, W3 = z.shape
    W = W3 // 3
    assert n_ctx == tr and n_lat % tr == 0
    n_lat_t = n_lat // tr
    hb = tr // HALO
    n_hb = L // HALO
    main = lambda col: pl.BlockSpec((None, tr, W), lambda b, i: (b, i, col))
    prev = lambda col: pl.BlockSpec((None, HALO, W), lambda b, i: (b, jnp.maximum(i * hb - 1, 0), col))
    nxt = lambda col: pl.BlockSpec((None, HALO, W), lambda b, i: (b, jnp.minimum((i + 1) * hb, n_hb - 1), col))
    full = lambda shape: pl.BlockSpec(shape, lambda b, i: tuple(0 for _ in shape))
    kern = functools.partial(_conv_pool_kernel, tr=tr, W=W, n_lat_t=n_lat_t, n_lat_rows=n_lat, n_ctx_rows=n_ctx)
    return pl.pallas_call(
        kern,
        grid=(B, L // tr),
        in_specs=[main(0), main(1), main(2), prev(0), prev(1), prev(2), nxt(0), nxt(1), nxt(2),
                  full((CONV_K, W)), full((1, W)), full((1, W)), full(pw.shape), full((1, W))],
        out_specs=pl.BlockSpec((None, tr, 2 * W), lambda b, i: (b, i, 0)),
        out_shape=jax.ShapeDtypeStruct((B, L, 2 * W), BF16),
        scratch_shapes=[pltpu.VMEM((SUBLANES, tr + 2 * HALO, W), F32),
                        pltpu.VMEM((SUBLANES, tr + 2 * HALO, W), F32),
                        pltpu.VMEM((tr, W), F32)],
        compiler_params=_cparams(("parallel", "arbitrary")),
        name="conv_pool",
    )(z, z, z, z, z, z, z, z, z, dw, lnw.reshape(1, W), lnb.reshape(1, W), pw, ps.reshape(1, W))


def _split(x):
    hi = x.astype(BF16)
    lo = (x - hi.astype(F32)).astype(BF16)
    return hi, lo


def _peer_q_kernel(x_ref, nw_ref, mod_ref, wh_ref, wl_ref, h_ref, q_ref, hh_ref, hl_ref,
                   *, tm, n_lat_rows, ks, ctx_row):
    b, i, j = pl.program_id(0), pl.program_id(1), pl.program_id(2)

    @pl.when(j == 0)
    def _():
        is_ctx = _is_ctx_col(i, tm, n_lat_rows)
        shift = _mod_rows(mod_ref, b, ks[0], is_ctx, ctx_row)
        scale = _mod_rows(mod_ref, b, ks[1], is_ctx, ctx_row)
        h = _norm_mod(x_ref[...], nw_ref[...], shift, scale)
        hi, lo = _split(h)
        hh_ref[...] = hi
        hl_ref[...] = lo
        h_ref[...] = hi

    hh = hh_ref[...]
    q_ref[...] = _dot(hh, wh_ref[...]) + (_dot(hl_ref[...], wh_ref[...]) + _dot(hh, wl_ref[...]))


def _peer_q(x, nw, mod, wh, wl, *, ks, n_lat_rows, tm, tn):
    B, L, D = x.shape
    n = wh.shape[1]
    kern = functools.partial(_peer_q_kernel, tm=tm, n_lat_rows=n_lat_rows, ks=ks, ctx_row=B)
    return pl.pallas_call(
        kern,
        grid=(B, L // tm, n // tn),
        in_specs=[
            pl.BlockSpec((None, tm, D), lambda b, i, j: (b, i, 0)),
            pl.BlockSpec((1, D), lambda b, i, j: (0, 0)),
            pl.BlockSpec(mod.shape, lambda b, i, j: (0, 0, 0)),
            pl.BlockSpec((D, tn), lambda b, i, j: (0, j)),
            pl.BlockSpec((D, tn), lambda b, i, j: (0, j)),
        ],
        out_specs=[pl.BlockSpec((None, tm, D), lambda b, i, j: (b, i, 0)),
                   pl.BlockSpec((None, tm, tn), lambda b, i, j: (b, i, j))],
        out_shape=[jax.ShapeDtypeStruct((B, L, D), BF16), jax.ShapeDtypeStruct((B, L, n), F32)],
        scratch_shapes=[pltpu.VMEM((tm, D), BF16), pltpu.VMEM((tm, D), BF16)],
        compiler_params=_cparams(("parallel", "parallel", "arbitrary")),
        name="peer_q",
    )(x, nw.reshape(1, D), mod, wh, wl)


def _bitonic_merge_desc(v):
    n = len(v)
    j = n // 2
    while j >= 1:
        for i in range(n):
            l = i ^ j
            if l > i:
                a, b = v[i], v[l]
                v[i], v[l] = jnp.maximum(a, b), jnp.minimum(a, b)
        j //= 2
    return v


def _bitonic_sort_desc(v):
    n = len(v)
    k = 2
    while k <= n:
        j = k // 2
        while j >= 1:
            for i in range(n):
                l = i ^ j
                if l > i:
                    a, b = v[i], v[l]
                    hi, lo = jnp.maximum(a, b), jnp.minimum(a, b)
                    if (i & k) == 0:
                        v[i], v[l] = hi, lo
                    else:
                        v[i], v[l] = lo, hi
            j //= 2
        k *= 2
    return v


def _merge_top(x, y):
    n = len(x)
    c = [x[i] if y[n - 1 - i] is None else jnp.maximum(x[i], y[n - 1 - i]) for i in range(n)]
    return _bitonic_merge_desc(c)


def _peer_topk_kernel(q_ref, kh_ref, kl_ref, thr_ref, c1_ref, s2_ref, e2_ref, s1_scr, top_scr,
                      *, tt, n_heads, n_keys, topk):
    U = SUBLANES
    nv = n_keys // U
    sub = lax.broadcasted_iota(jnp.int32, (U, tt), 0)
    ap = [None] * topk
    bp = [None] * topk
    for h in range(n_heads):
        for p in range(2):
            c0 = (h * 2 + p) * n_keys
            qh, ql = _split(q_ref[:, c0:c0 + n_keys])
            kh, kl = kh_ref[h, p], kl_ref[h, p]
            st = _dot_nt(kh, qh) + (_dot_nt(kh, ql) + _dot_nt(kl, qh))
            if p == 0:
                s1_scr[h] = st
            else:
                s2_ref[h * n_keys:(h + 1) * n_keys, :] = st
            v = _bitonic_sort_desc([st[U * i:U * (i + 1), :] for i in range(nv)])
            for s in (4, 2, 1):
                v = _merge_top(v, [pltpu.roll(x, s, 0) for x in v])
            for a in range(topk):
                top_scr[h, p, a] = v[a]
                pk = ap if p == 0 else bp
                pk[a] = v[a] if h == 0 else jnp.where(sub == h, v[a], pk[a])
    acc = [ap[0] + bp[b] for b in range(topk)]
    for a in range(1, topk):
        na = topk // (a + 1)
        ya = [ap[a] + bp[b] if b < na else None for b in range(topk)]
        acc = _merge_top(acc, ya)
    v16p = acc[topk - 1]
    zp = jnp.zeros((U, tt), F32)
    for a in range(topk):
        zp = zp + jnp.exp(acc[a] - acc[0])
    rzp = 1.0 / zp
    for h in range(n_heads):
        v16 = jnp.broadcast_to(v16p[h:h + 1, :], (U, tt))
        rz = jnp.broadcast_to(rzp[h:h + 1, :], (U, tt))
        m1 = top_scr[h, 0, 0]
        m2 = top_scr[h, 1, 0]
        bs = [top_scr[h, 1, b] for b in range(topk)]
        for i in range(nv):
            rs = slice(h * n_keys + U * i, h * n_keys + U * (i + 1))
            x1 = s1_scr[h, U * i:U * (i + 1), :]
            thr = jnp.full((U, tt), jnp.inf, F32)
            for b in range(topk):
                thr = jnp.where(x1 + bs[b] >= v16, bs[b], thr)
            thr_ref[rs, :] = thr
            c1_ref[rs, :] = jnp.exp(x1 - m1) * rz
            e2_ref[rs, :] = jnp.exp(s2_ref[rs, :] - m2)


def _peer_topk(q, kh, kl, *, tt):
    B, L, _ = q.shape
    n_heads, _, n_keys, dk = kh.shape
    rows = n_heads * n_keys
    kern = functools.partial(_peer_topk_kernel, tt=tt, n_heads=n_heads, n_keys=n_keys, topk=PEER_TOPK)
    out = jax.ShapeDtypeStruct((B, rows, L), F32)
    ospec = pl.BlockSpec((None, rows, tt), lambda b, i: (b, 0, i))
    return pl.pallas_call(
        kern,
        grid=(B, L // tt),
        in_specs=[
            pl.BlockSpec((None, tt, q.shape[2]), lambda b, i: (b, i, 0)),
            pl.BlockSpec(kh.shape, lambda b, i: (0, 0, 0, 0)),
            pl.BlockSpec(kl.shape, lambda b, i: (0, 0, 0, 0)),
        ],
        out_specs=[ospec] * 4,
        out_shape=[out] * 4,
        scratch_shapes=[pltpu.VMEM((n_heads, n_keys, tt), F32),
                        pltpu.VMEM((n_heads, 2, PEER_TOPK, SUBLANES, tt), F32)],
        compiler_params=_cparams(("parallel", "arbitrary")),
        name="peer_topk",
    )(q, kh, kl)


PEER_RG = 16


def _peer_kernel(h_ref, u_ref, vt_ref, thr_ref, c1_ref, s2_ref, e2_ref, x_ref, mod_ref, o_ref,
                 acc_ref, st_ref, a_ref, *, tm, te, n_e, n_lat_rows, n_heads, n_keys, ctx_row):
    b, i, e = pl.program_id(0), pl.program_id(1), pl.program_id(2)

    @pl.when(e == 0)
    def _():
        acc_ref[...] = jnp.zeros_like(acc_ref)

    st_ref[...] = _dot_nt(u_ref[...], h_ref[...])
    n_i1 = te // n_keys
    for s in range(n_i1):
        i1 = e * n_i1 + s
        thr_rows = [thr_ref[pl.ds(hh * n_keys + i1, 1), :] for hh in range(n_heads)]
        c1_rows = [c1_ref[pl.ds(hh * n_keys + i1, 1), :] for hh in range(n_heads)]

        def rows(r, carry):
            r0 = pl.multiple_of(r * PEER_RG, PEER_RG)
            w = jnp.zeros((PEER_RG, tm), F32)
            for hh in range(n_heads):
                s2 = s2_ref[pl.ds(hh * n_keys + r0, PEER_RG), :]
                e2 = e2_ref[pl.ds(hh * n_keys + r0, PEER_RG), :]
                w = w + jnp.where(s2 >= thr_rows[hh], e2, 0.0) * c1_rows[hh]
            sv = st_ref[pl.ds(s * n_keys + r0, PEER_RG), :]
            g = 0.5 * sv * (1.0 + lax.erf(sv * INV_SQRT2))
            a_ref[pl.ds(s * n_keys + r0, PEER_RG), :] = (g * w).astype(BF16)
            return carry

        lax.fori_loop(0, n_keys // PEER_RG, rows, 0)

    acc_ref[...] += _dot(vt_ref[...], a_ref[...])

    @pl.when(e == n_e - 1)
    def _():
        is_ctx = _is_ctx_col(i, tm, n_lat_rows)
        gate = jnp.where(is_ctx, mod_ref[ctx_row], mod_ref[b])
        o_ref[...] = x_ref[...] + gate * acc_ref[...].T


def _peer_main(h, u, vt, thr, c1, s2, e2, x, mod_gate, *, n_lat_rows, tm, te):
    B, L, D = x.shape
    E = u.shape[0]
    rows = thr.shape[1]
    n_e = E // te
    kern = functools.partial(_peer_kernel, tm=tm, te=te, n_e=n_e, n_lat_rows=n_lat_rows,
                             n_heads=PEER_HEADS, n_keys=PEER_KEYS, ctx_row=B)
    tok = pl.BlockSpec((None, rows, tm), lambda b, i, e: (b, 0, i))
    return pl.pallas_call(
        kern,
        grid=(B, L // tm, n_e),
        in_specs=[
            pl.BlockSpec((None, tm, D), lambda b, i, e: (b, i, 0)),
            pl.BlockSpec((te, D), lambda b, i, e: (e, 0)),
            pl.BlockSpec((D, te), lambda b, i, e: (0, e)),
            tok, tok, tok, tok,
            pl.BlockSpec((None, tm, D), lambda b, i, e: (b, i, 0)),
            pl.BlockSpec(mod_gate.shape, lambda b, i, e: (0, 0, 0)),
        ],
        out_specs=pl.BlockSpec((None, tm, D), lambda b, i, e: (b, i, 0)),
        out_shape=jax.ShapeDtypeStruct((B, L, D), F32),
        scratch_shapes=[pltpu.VMEM((D, tm), F32), pltpu.VMEM((te, tm), F32), pltpu.VMEM((te, tm), BF16)],
        compiler_params=_cparams(("parallel", "parallel", "arbitrary")),
        name="peer_main",
    )(h, u, vt, thr, c1, s2, e2, x, mod_gate)


def _final_norm_kernel(x_ref, w_ref, o_ref):
    x = x_ref[...]
    ms = jnp.mean(x * x, axis=-1, keepdims=True)
    o_ref[...] = x * lax.rsqrt(ms + EPS) * w_ref[...]


def _final_norm(x, w, *, n_lat, tr):
    B, _, D = x.shape
    return pl.pallas_call(
        _final_norm_kernel,
        grid=(B, n_lat // tr),
        in_specs=[pl.BlockSpec((None, tr, D), lambda b, i: (b, i, 0)),
                  pl.BlockSpec((1, D), lambda b, i: (0, 0))],
        out_specs=pl.BlockSpec((None, tr, D), lambda b, i: (b, i, 0)),
        out_shape=jax.ShapeDtypeStruct((B, n_lat, D), F32),
        compiler_params=_cparams(("parallel", "arbitrary")),
        name="final_norm",
    )(x, w.reshape(1, D))


TILE_ROWS = 256
TILE_MM = 768
TILE_N = 512
TILE_PEER_TOK = 384
TILE_PEER_EXP = 512
TILE_TOPK_TOK = 256
TILE_MOD_N = 1024


def kernel(x, c, ctx, c_ctx, ada_w, ada_b, norm1_w, norm2_w, final_norm_w, ev_in_w, ev_out_w,
           hg_lb_logits, hg_norm_w, od_in_w, od_out_w, conv_dw, conv_ln_w, conv_ln_b, pool_w,
           pool_scale, peer_q_w, peer_keys, peer_u, peer_v):
    B, S, D = x.shape
    n_ctx = ctx.shape[1]
    depth = ada_w.shape[0]
    fw = ev_out_w.shape[1] // 2
    assert n_ctx == TILE_ROWS and S % TILE_ROWS == 0

    pos = _sincos_table(S // GRID_W, D)
    X = _embed(x, pos, ctx, TILE_ROWS)

    rows = SUBLANES
    cvec = jnp.zeros((rows, D), F32).at[:B].set(c).at[B].set(c_ctx)
    mods = _modulation(cvec, ada_w, ada_b, TILE_MOD_N)

    lb_p = jax.nn.softmax(hg_lb_logits.astype(F32), axis=0)
    lower = jnp.concatenate([jnp.zeros_like(lb_p[:1]), jnp.cumsum(lb_p[1:], axis=0)], axis=0)
    lower = jnp.clip(lower, 0.0, 1.0)

    mm = dict(n_lat_rows=S, tm=TILE_MM, tn=TILE_N)
    for l in range(depth):
        mod = mods[l].reshape(rows, N_MOD, D)
        j = l // 2
        if l % 2 == 0:
            w_in = ev_in_w[j].astype(BF16)
            a = _norm_matmul(X, norm1_w[l], mod, w_in[:, :fw], ks=(0, 1), **mm)
            z = _norm_matmul(X, norm1_w[l], mod, w_in[:, fw:], ks=(0, 1), **mm)
            four = _fourier_mix(a, S, n_ctx)
            of = _hgrn(z, lower[j], reverse=False, n_lat=S, n_ctx=n_ctx, tc=TILE_ROWS, col0=0)
            ob = _hgrn(z, lower[j], reverse=True, n_lat=S, n_ctx=n_ctx, tc=TILE_ROWS, col0=0)
            mix = _hg_combine(of, ob, z, four, hg_norm_w[j], gcol=4, tm=TILE_ROWS)
            w_out = ev_out_w[j].astype(BF16)
        else:
            z = _norm_matmul(X, norm1_w[l], mod, od_in_w[j].astype(BF16), ks=(0, 1), **mm)
            mix = _conv_pool(z, conv_dw[j], conv_ln_w[j], conv_ln_b[j], pool_w[j].astype(BF16),
                             pool_scale[j], n_lat=S, n_ctx=n_ctx, tr=TILE_ROWS)
            w_out = od_out_w[j].astype(BF16)
        X = _out_proj(mix, w_out, X, mod[:, 2:3, :], **mm)

        qw = peer_q_w[l]
        qwh, qwl = _split(qw)
        h, q = _peer_q(X, norm2_w[l], mod, qwh, qwl, ks=(3, 4), n_lat_rows=S, tm=TILE_PEER_TOK, tn=TILE_N)
        kh, kl = _split(peer_keys[l])
        thr, c1, s2, e2 = _peer_topk(q, kh, kl, tt=TILE_TOPK_TOK)
        X = _peer_main(h, peer_u[l].astype(BF16), peer_v[l].astype(BF16).T, thr, c1, s2, e2, X,
                       mod[:, 5:6, :], n_lat_rows=S, tm=TILE_PEER_TOK, te=TILE_PEER_EXP)

    return _final_norm(X, final_norm_w, n_lat=S, tr=TILE_ROWS)
```

```python
import functools
import math

import numpy as np
import jax
import jax.numpy as jnp
from jax import lax
from jax.experimental import pallas as pl
from jax.experimental.pallas import tpu as pltpu

F32 = jnp.float32
BF16 = jnp.bfloat16

GRID_W = 64
FNET_GROUPS = 4
HG_HEAD_W = 128
F_MIN = 1e-30
CONV_K = 31
POOL_WINDOWS = (2, 4, 8, 16)
PEER_KEYS = 128
PEER_HEADS = 8
PEER_TOPK = 16
N_MOD = 6
EPS = 1e-6

LANES = 128
SUBLANES = 8
VMEM_LIMIT_BYTES = 56 * 1024 * 1024

INV_SQRT2 = 1.0 / math.sqrt(2.0)


def _cparams(sem):
    return pltpu.CompilerParams(dimension_semantics=sem, vmem_limit_bytes=VMEM_LIMIT_BYTES)


def _dot(a, b):
    return jnp.dot(a, b, preferred_element_type=F32)


def _dot_nt(a, b):
    return lax.dot_general(a, b, (((1,), (1,)), ((), ())), preferred_element_type=F32)


def _dot_tn(a, b):
    return lax.dot_general(a, b, (((0,), (0,)), ((), ())), preferred_element_type=F32)


def _silu(x):
    return x * jax.nn.sigmoid(x)


def _embed_kernel(x_ref, pos_ref, ctx_ref, o_ref, *, n_lat):
    i = pl.program_id(1)

    @pl.when(i < n_lat)
    def _():
        o_ref[...] = x_ref[...] + pos_ref[...]

    @pl.when(i >= n_lat)
    def _():
        o_ref[...] = ctx_ref[...]


def _embed(x, pos, ctx, tr):
    B, S, D = x.shape
    nc = ctx.shape[1]
    n_lat, n_ctx = S // tr, nc // tr
    return pl.pallas_call(
        functools.partial(_embed_kernel, n_lat=n_lat),
        grid=(B, n_lat + n_ctx),
        in_specs=[
            pl.BlockSpec((None, tr, D), lambda b, i: (b, jnp.minimum(i, n_lat - 1), 0)),
            pl.BlockSpec((tr, D), lambda b, i: (jnp.minimum(i, n_lat - 1), 0)),
            pl.BlockSpec((None, tr, D), lambda b, i: (b, jnp.maximum(i - n_lat, 0), 0)),
        ],
        out_specs=pl.BlockSpec((None, tr, D), lambda b, i: (b, i, 0)),
        out_shape=jax.ShapeDtypeStruct((B, S + nc, D), F32),
        compiler_params=_cparams(("parallel", "arbitrary")),
        name="embed",
    )(x, pos, ctx)


def _sincos_table(rows, dim):
    quarter = dim // 4
    row = jnp.repeat(jnp.arange(rows), GRID_W).astype(F32)[:, None]
    col = jnp.tile(jnp.arange(GRID_W), rows).astype(F32)[:, None]
    omega = 1.0 / (10000.0 ** (jnp.arange(quarter, dtype=F32) / quarter))
    ar, ac = row * omega, col * omega
    return jnp.concatenate([jnp.sin(ar), jnp.cos(ar), jnp.sin(ac), jnp.cos(ac)], axis=-1)


def _mod_kernel(c_ref, w_ref, b_ref, o_ref):
    sc = _silu(c_ref[...]).astype(BF16)
    o_ref[...] = _dot(sc, w_ref[...].astype(BF16)) + b_ref[...]


def _modulation(cvec, ada_w, ada_b, tn):
    depth, D, n = ada_w.shape
    rows = cvec.shape[0]
    return pl.pallas_call(
        _mod_kernel,
        grid=(depth, n // tn),
        in_specs=[
            pl.BlockSpec((rows, D), lambda l, j: (0, 0)),
            pl.BlockSpec((None, D, tn), lambda l, j: (l, 0, j)),
            pl.BlockSpec((None, 1, tn), lambda l, j: (l, 0, j)),
        ],
        out_specs=pl.BlockSpec((None, rows, tn), lambda l, j: (l, 0, j)),
        out_shape=jax.ShapeDtypeStruct((depth, rows, n), F32),
        compiler_params=_cparams(("parallel", "arbitrary")),
        name="modulation",
    )(cvec, ada_w, ada_b.reshape(depth, 1, n))


def _mod_rows(mod_ref, b, k, is_ctx, ctx_row):
    lat = mod_ref[b, pl.ds(k, 1), :]
    cx = mod_ref[ctx_row, pl.ds(k, 1), :]
    return jnp.where(is_ctx, cx, lat)


def _norm_mod(x, nw, shift, scale):
    ms = jnp.mean(x * x, axis=-1, keepdims=True)
    y = x * lax.rsqrt(ms + EPS) * nw
    return y * (1.0 + scale) + shift


def _is_ctx_col(i, tm, n_lat_rows):
    rows = i * tm + lax.broadcasted_iota(jnp.int32, (tm, 1), 0)
    return rows >= n_lat_rows


def _norm_matmul_kernel(x_ref, nw_ref, mod_ref, w_ref, o_ref, h_ref, *, tm, n_lat_rows, ks, ctx_row):
    b, i, j = pl.program_id(0), pl.program_id(1), pl.program_id(2)

    @pl.when(j == 0)
    def _():
        is_ctx = _is_ctx_col(i, tm, n_lat_rows)
        shift = _mod_rows(mod_ref, b, ks[0], is_ctx, ctx_row)
        scale = _mod_rows(mod_ref, b, ks[1], is_ctx, ctx_row)
        h_ref[...] = _norm_mod(x_ref[...], nw_ref[...], shift, scale).astype(BF16)

    o_ref[...] = _dot(h_ref[...], w_ref[...]).astype(o_ref.dtype)


def _norm_matmul(x, nw, mod, w, *, ks, n_lat_rows, tm, tn, out_dtype=F32):
    B, L, D = x.shape
    n = w.shape[1]
    kern = functools.partial(_norm_matmul_kernel, tm=tm, n_lat_rows=n_lat_rows, ks=ks, ctx_row=B)
    return pl.pallas_call(
        kern,
        grid=(B, L // tm, n // tn),
        in_specs=[
            pl.BlockSpec((None, tm, D), lambda b, i, j: (b, i, 0)),
            pl.BlockSpec((1, D), lambda b, i, j: (0, 0)),
            pl.BlockSpec(mod.shape, lambda b, i, j: (0, 0, 0)),
            pl.BlockSpec((D, tn), lambda b, i, j: (0, j)),
        ],
        out_specs=pl.BlockSpec((None, tm, tn), lambda b, i, j: (b, i, j)),
        out_shape=jax.ShapeDtypeStruct((B, L, n), out_dtype),
        scratch_shapes=[pltpu.VMEM((tm, D), BF16)],
        compiler_params=_cparams(("parallel", "parallel", "arbitrary")),
        name="norm_matmul",
    )(x, nw.reshape(1, D), mod, w)


def _out_proj_kernel(m_ref, w_ref, x_ref, gate_ref, o_ref, *, tm, n_lat_rows, ctx_row):
    b, i = pl.program_id(0), pl.program_id(1)
    is_ctx = _is_ctx_col(i, tm, n_lat_rows)
    gate = jnp.where(is_ctx, gate_ref[ctx_row], gate_ref[b])
    o_ref[...] = x_ref[...] + gate * _dot(m_ref[...], w_ref[...])


def _out_proj(m, w, x, mod_gate, *, n_lat_rows, tm, tn):
    B, L, D = x.shape
    K = m.shape[2]
    kern = functools.partial(_out_proj_kernel, tm=tm, n_lat_rows=n_lat_rows, ctx_row=B)
    return pl.pallas_call(
        kern,
        grid=(B, L // tm, D // tn),
        in_specs=[
            pl.BlockSpec((None, tm, K), lambda b, i, j: (b, i, 0)),
            pl.BlockSpec((K, tn), lambda b, i, j: (0, j)),
            pl.BlockSpec((None, tm, tn), lambda b, i, j: (b, i, j)),
            pl.BlockSpec((mod_gate.shape[0], 1, tn), lambda b, i, j: (0, 0, j)),
        ],
        out_specs=pl.BlockSpec((None, tm, tn), lambda b, i, j: (b, i, j)),
        out_shape=jax.ShapeDtypeStruct((B, L, D), F32),
        compiler_params=_cparams(("parallel", "parallel", "arbitrary")),
        name="out_proj",
    )(m, w, x, mod_gate)


def _dft_mats(n):
    k = np.arange(n)
    ang = 2.0 * np.pi * ((k[:, None] * k[None, :]) % n) / n
    return np.cos(ang), np.sin(ang)


def _fa_kernel(x_ref, c_ref, s_ref, tc_ref, ts_ref, vr_ref, vi_ref, *, n_sub, cw):
    cm, sm = c_ref[...], s_ref[...]
    for s in range(n_sub):
        sl = slice(s * cw, (s + 1) * cw)
        xs = x_ref[:, sl].astype(BF16)
        a = _dot(cm, xs)
        bneg = _dot(sm, xs)
        tc = tc_ref[:, s:s + 1]
        ts = ts_ref[:, s:s + 1]
        vr_ref[:, sl] = (a * tc - bneg * ts).astype(BF16)
        vi_ref[:, sl] = (-bneg * tc - a * ts).astype(BF16)


def _fb_kernel(vr_ref, vi_ref, c_ref, s_ref, cc_ref, sc_ref, o_ref, *, groups, gw, scale):
    cm, sm = c_ref[...], s_ref[...]
    vr, vi = vr_ref[...], vi_ref[...]
    ur = (_dot(cm, vr) + _dot(sm, vi)).astype(BF16)
    ui = (_dot(cm, vi) - _dot(sm, vr)).astype(BF16)
    cc, sc = cc_ref[...], sc_ref[...]
    for g in range(groups):
        sl = slice(g * gw, (g + 1) * gw)
        o = _dot(ur[:, sl], cc) + _dot(ui[:, sl], sc)
        o_ref[:, sl] = (o * scale).astype(o_ref.dtype)


def _fctx_kernel(x_ref, c_ref, s_ref, cc_ref, sc_ref, o_ref, *, groups, gw, scale):
    x = x_ref[...].astype(BF16)
    ur = _dot(c_ref[...], x).astype(BF16)
    uineg = _dot(s_ref[...], x).astype(BF16)
    cc, sc = cc_ref[...], sc_ref[...]
    for g in range(groups):
        sl = slice(g * gw, (g + 1) * gw)
        o = _dot(ur[:, sl], cc) - _dot(uineg[:, sl], sc)
        o_ref[:, sl] = (o * scale).astype(o_ref.dtype)


def _fourier_mix(a, n_lat, n_ctx):
    B, L, W = a.shape
    gw = W // FNET_GROUPS
    n1, n2 = 128, n_lat // 128
    assert n1 * n2 == n_lat and L == n_lat + n_ctx and n_lat % n_ctx == 0
    c2, s2 = _dft_mats(n2)
    c1, s1 = _dft_mats(n1)
    cc, sc = _dft_mats(gw)
    cx, sx = _dft_mats(n_ctx)
    n_sub = 8
    tw_ang = 2.0 * np.pi * ((np.arange(n2)[:, None] * np.arange(n1)[None, :]) % n_lat) / n_lat
    tc = np.cos(tw_ang).reshape(n2, n1 // n_sub, n_sub).transpose(1, 0, 2)
    ts = np.sin(tw_ang).reshape(n2, n1 // n_sub, n_sub).transpose(1, 0, 2)
    bf = lambda m: jnp.asarray(m, dtype=BF16)
    f32 = lambda m: jnp.asarray(m, dtype=F32)

    blk = n_sub * W
    a_v = a.reshape(B, L // n1, n1 * W)
    full2 = lambda shape: pl.BlockSpec(shape, lambda b, j: (0, 0))
    vr, vi = pl.pallas_call(
        functools.partial(_fa_kernel, n_sub=n_sub, cw=W),
        grid=(B, n1 // n_sub),
        in_specs=[
            pl.BlockSpec((None, n2, blk), lambda b, j: (b, 0, j)),
            full2((n2, n2)), full2((n2, n2)),
            pl.BlockSpec((None, n2, n_sub), lambda b, j: (j, 0, 0)),
            pl.BlockSpec((None, n2, n_sub), lambda b, j: (j, 0, 0)),
        ],
        out_specs=[pl.BlockSpec((None, n2, blk), lambda b, j: (b, 0, j))] * 2,
        out_shape=[jax.ShapeDtypeStruct((B, n2, n1 * W), BF16)] * 2,
        compiler_params=_cparams(("parallel", "arbitrary")),
        name="fourier_a",
    )(a_v, bf(c2), bf(s2), f32(tc), f32(ts))

    vr = vr.reshape(B, n_lat, W)
    vi = vi.reshape(B, n_lat, W)
    lat = pl.pallas_call(
        functools.partial(_fb_kernel, groups=FNET_GROUPS, gw=gw, scale=1.0 / math.sqrt(n_lat * gw)),
        grid=(B, n2),
        in_specs=[
            pl.BlockSpec((None, n1, W), lambda b, k: (b, k, 0)),
            pl.BlockSpec((None, n1, W), lambda b, k: (b, k, 0)),
            full2((n1, n1)), full2((n1, n1)), full2((gw, gw)), full2((gw, gw)),
        ],
        out_specs=pl.BlockSpec((None, n1, W), lambda b, k: (b, 0, k)),
        out_shape=jax.ShapeDtypeStruct((B, n1, n2 * W), BF16),
        compiler_params=_cparams(("parallel", "arbitrary")),
        name="fourier_b",
    )(vr, vi, bf(c1), bf(s1), bf(cc), bf(sc))
    lat = lat.reshape(B, n_lat, W)

    full1 = lambda shape: pl.BlockSpec(shape, lambda b: (0, 0))
    cxo = pl.pallas_call(
        functools.partial(_fctx_kernel, groups=FNET_GROUPS, gw=gw, scale=1.0 / math.sqrt(n_ctx * gw)),
        grid=(B,),
        in_specs=[
            pl.BlockSpec((None, n_ctx, W), lambda b: (b, n_lat // n_ctx, 0)),
            full1((n_ctx, n_ctx)), full1((n_ctx, n_ctx)), full1((gw, gw)), full1((gw, gw)),
        ],
        out_specs=pl.BlockSpec((None, n_ctx, W), lambda b: (b, 0, 0)),
        out_shape=jax.ShapeDtypeStruct((B, n_ctx, W), BF16),
        compiler_params=_cparams(("parallel",)),
        name="fourier_ctx",
    )(a, bf(cx), bf(sx), bf(cc), bf(sc))
    return jnp.concatenate([lat, cxo], axis=1)


def _hgrn_kernel(q_ref, f_ref, v_ref, lb_ref, o_ref, st_ref, *, reverse, n_units, n_heads):
    c = pl.program_id(1)

    @pl.when(c == 0)
    def _():
        st_ref[...] = jnp.zeros_like(st_ref)

    U = SUBLANES
    hw = HG_HEAD_W
    row = lax.broadcasted_iota(jnp.int32, (U, hw), 0)
    rowc = lax.broadcasted_iota(jnp.int32, (U, 1), 0)

    def unit(u, carry):
        uu = (n_units - 1 - u) if reverse else u
        r0 = pl.multiple_of(uu * U, U)
        for h in range(n_heads):
            sl = slice(h * hw, (h + 1) * hw)
            q8 = q_ref[pl.ds(r0, U), sl]
            f8 = f_ref[pl.ds(r0, U), sl]
            v8 = v_ref[pl.ds(r0, U), sl]
            lb = lb_ref[:, sl]
            sig = jax.nn.sigmoid(f8)
            fe = jnp.maximum(lb + (1.0 - lb) * sig, F_MIN)
            kk = (1.0 - lb) * (1.0 - sig)
            qs = _silu(q8)
            pre, suf = fe, fe
            for s in (1, 2, 4):
                pre = pre * jnp.where(row >= s, pltpu.roll(pre, s, 0), 1.0)
                suf = suf * jnp.where(row <= U - 1 - s, pltpu.roll(suf, U - s, 0), 1.0)
            if not reverse:
                qp = qs * pre
                kp = kk * jnp.where(row <= U - 2, pltpu.roll(suf, U - 1, 0), 1.0)
                dec = pre[U - 1:U, :]
            else:
                qp = qs * suf
                kp = kk * jnp.where(row >= 1, pltpu.roll(pre, 1, 0), 1.0)
                dec = suf[0:1, :]
            o = jnp.sum(qs * kk, axis=-1, keepdims=True) * v8
            d = None
            for lag in range(1, U):
                sh = lag if not reverse else U - lag
                shf = (lag - 1) if not reverse else (U - (lag - 1)) % U
                fsh = fe if shf == 0 else pltpu.roll(fe, shf, 0)
                d = fsh if d is None else d * fsh
                kd = pltpu.roll(kk, sh, 0)
                vd = pltpu.roll(v8, sh, 0)
                rs = jnp.sum(qs * kd * d, axis=-1, keepdims=True)
                ok = (rowc >= lag) if not reverse else (rowc <= U - 1 - lag)
                o = o + jnp.where(ok, rs, 0.0) * vd
            st = st_ref[h]
            o = o + _dot_nt(qp.astype(BF16), st.astype(BF16))
            o_ref[pl.ds(r0, U), sl] = o
            st_ref[h] = st * dec + _dot_tn(v8.astype(BF16), kp.astype(BF16))
        return carry

    lax.fori_loop(0, n_units, unit, 0)


def _hgrn(z, lb, *, reverse, n_lat, n_ctx, tc, col0):
    B, L, _ = z.shape
    W = lb.shape[-1]
    n_heads = W // HG_HEAD_W
    n_lat_c, n_ctx_c = n_lat // tc, n_ctx // tc
    n_chunks = n_lat_c + n_ctx_c

    def chunk(c):
        if not reverse:
            return jnp.where(c < n_ctx_c, n_lat_c + c, c - n_ctx_c)
        return jnp.where(c < n_ctx_c, n_chunks - 1 - c, n_chunks - 1 - c)

    fcol = col0 + (2 if reverse else 1)
    kern = functools.partial(_hgrn_kernel, reverse=reverse, n_units=tc // SUBLANES, n_heads=n_heads)
    return pl.pallas_call(
        kern,
        grid=(B, n_chunks),
        in_specs=[
            pl.BlockSpec((None, tc, W), lambda b, c: (b, chunk(c), col0)),
            pl.BlockSpec((None, tc, W), lambda b, c: (b, chunk(c), fcol)),
            pl.BlockSpec((None, tc, W), lambda b, c: (b, chunk(c), col0 + 3)),
            pl.BlockSpec((1, W), lambda b, c: (0, 0)),
        ],
        out_specs=pl.BlockSpec((None, tc, W), lambda b, c: (b, chunk(c), 0)),
        out_shape=jax.ShapeDtypeStruct((B, L, W), F32),
        scratch_shapes=[pltpu.VMEM((n_heads, HG_HEAD_W, HG_HEAD_W), F32)],
        compiler_params=_cparams(("parallel", "arbitrary")),
        name="hgrn_bwd" if reverse else "hgrn_fwd",
    )(z, z, z, lb.reshape(1, W))


def _hg_combine_kernel(of_ref, ob_ref, g_ref, four_ref, nw_ref, m_ref, *, n_heads, W):
    hw = HG_HEAD_W
    nw = nw_ref[...]
    m_ref[:, :W] = four_ref[...]
    for h in range(n_heads):
        sl = slice(h * hw, (h + 1) * hw)
        o = of_ref[:, sl] + ob_ref[:, sl]
        ms = jnp.mean(o * o, axis=-1, keepdims=True)
        y = o * lax.rsqrt(ms + EPS) * nw * _silu(g_ref[:, sl])
        m_ref[:, W + h * hw:W + (h + 1) * hw] = y.astype(BF16)


def _hg_combine(of, ob, z, four, nw, *, gcol, tm):
    B, L, W = of.shape
    kern = functools.partial(_hg_combine_kernel, n_heads=W // HG_HEAD_W, W=W)
    return pl.pallas_call(
        kern,
        grid=(B, L // tm),
        in_specs=[
            pl.BlockSpec((None, tm, W), lambda b, i: (b, i, 0)),
            pl.BlockSpec((None, tm, W), lambda b, i: (b, i, 0)),
            pl.BlockSpec((None, tm, W), lambda b, i: (b, i, gcol)),
            pl.BlockSpec((None, tm, W), lambda b, i: (b, i, 0)),
            pl.BlockSpec((1, HG_HEAD_W), lambda b, i: (0, 0)),
        ],
        out_specs=pl.BlockSpec((None, tm, 2 * W), lambda b, i: (b, i, 0)),
        out_shape=jax.ShapeDtypeStruct((B, L, 2 * W), BF16),
        compiler_params=_cparams(("parallel", "arbitrary")),
        name="hg_combine",
    )(of, ob, z, four, nw.reshape(1, HG_HEAD_W))


HALO = 16


def _conv_pool_kernel(a_ref, b_ref, p_ref, ap_ref, bp_ref, pp_ref, an_ref, bn_ref, pn_ref,
                      dw_ref, lnw_ref, lnb_ref, pw_ref, ps_ref, o_ref, ue_ref, pe_ref, cv_ref,
                      *, tr, W, n_lat_t, n_lat_rows, n_ctx_rows):
    i = pl.program_id(1)
    is_ctx_t = i >= n_lat_t
    first = jnp.logical_or(i == 0, i == n_lat_t)
    last = jnp.logical_or(i == n_lat_t - 1, is_ctx_t)
    pv = jnp.where(first, 0.0, 1.0)
    nv = jnp.where(last, 0.0, 1.0)

    ext = tr + 2 * HALO
    ue_ref[0, 0:HALO, :] = ap_ref[...] * jax.nn.sigmoid(bp_ref[...]) * pv
    ue_ref[0, HALO:HALO + tr, :] = a_ref[...] * jax.nn.sigmoid(b_ref[...])
    ue_ref[0, HALO + tr:ext, :] = an_ref[...] * jax.nn.sigmoid(bn_ref[...]) * nv
    pe_ref[0, 0:HALO, :] = pp_ref[...] * pv
    pe_ref[0, HALO:HALO + tr, :] = p_ref[...]
    pe_ref[0, HALO + tr:ext, :] = pn_ref[...] * nv
    CP = 2 * HALO
    for r in range(1, SUBLANES):
        for c0 in range(0, ext - SUBLANES, CP):
            n = min(CP, ext - SUBLANES - c0)
            ue_ref[r, c0:c0 + n, :] = ue_ref[0, c0 + r:c0 + r + n, :]
            pe_ref[r, c0:c0 + n, :] = pe_ref[0, c0 + r:c0 + r + n, :]

    RC = 32
    half = CONV_K // 2

    def shifted(ref, r0, off, ls):
        return ref[off % SUBLANES, pl.ds(pl.multiple_of(r0 + (off - off % SUBLANES), SUBLANES), RC), ls]

    def conv_rows(rc, carry):
        r0 = pl.multiple_of(rc * RC, RC)
        for lc in range(W // LANES):
            ls = slice(lc * LANES, (lc + 1) * LANES)
            acc = jnp.zeros((RC, LANES), F32)
            for j in range(CONV_K):
                acc = acc + shifted(ue_ref, r0, HALO - half + j, ls) * dw_ref[j:j + 1, ls]
            cv_ref[pl.ds(r0, RC), ls] = acc
        return carry

    lax.fori_loop(0, tr // RC, conv_rows, 0)

    seg_len = jnp.where(is_ctx_t, n_ctx_rows, n_lat_rows)
    t_base = jnp.where(is_ctx_t, 0, i * tr)
    gw = W // len(POOL_WINDOWS)

    def norm_pool_rows(rc, carry):
        r0 = pl.multiple_of(rc * RC, RC)
        u = cv_ref[pl.ds(r0, RC), :]
        mu = jnp.mean(u, axis=-1, keepdims=True)
        xc = u - mu
        var = jnp.mean(xc * xc, axis=-1, keepdims=True)
        y = _silu(xc * lax.rsqrt(var + EPS) * lnw_ref[...] + lnb_ref[...])
        o_ref[pl.ds(r0, RC), 0:W] = y.astype(BF16)
        t_loc = t_base + r0 + lax.broadcasted_iota(jnp.int32, (RC, 1), 0)
        for g, win in enumerate(POOL_WINDOWS):
            ls = slice(g * gw, (g + 1) * gw)
            acc = jnp.zeros((RC, gw), F32)
            for k in range(win):
                acc = acc + shifted(pe_ref, r0, HALO - win // 2 + k, ls)
            lo = jnp.maximum(t_loc - win // 2, 0)
            hi = jnp.minimum(t_loc + (win - win // 2), seg_len)
            cnt = (hi - lo).astype(F32)
            cv_ref[pl.ds(r0, RC), ls] = acc / cnt - shifted(pe_ref, r0, HALO, ls)
        return carry

    lax.fori_loop(0, tr // RC, norm_pool_rows, 0)

    for g in range(len(POOL_WINDOWS)):
        ls = slice(g * gw, (g + 1) * gw)
        pm = _dot(cv_ref[:, ls].astype(BF16), pw_ref[g]) * ps_ref[:, ls]
        o_ref[:, W + g * gw:W + (g + 1) * gw] = pm.astype(BF16)


def _conv_pool(z, dw, lnw, lnb, pw, ps, *, n_lat, n_ctx, tr):
    B, L, W3 = z.shape
    W = W3 // 3
    assert n_ctx == tr and n_lat % tr == 0
    n_lat_t = n_lat // tr
    hb = tr // HALO
    n_hb = L // HALO
    main = lambda col: pl.BlockSpec((None, tr, W), lambda b, i: (b, i, col))
    prev = lambda col: pl.BlockSpec((None, HALO, W), lambda b, i: (b, jnp.maximum(i * hb - 1, 0), col))
    nxt = lambda col: pl.BlockSpec((None, HALO, W), lambda b, i: (b, jnp.minimum((i + 1) * hb, n_hb - 1), col))
    full = lambda shape: pl.BlockSpec(shape, lambda b, i: tuple(0 for _ in shape))
    kern = functools.partial(_conv_pool_kernel, tr=tr, W=W, n_lat_t=n_lat_t, n_lat_rows=n_lat, n_ctx_rows=n_ctx)
    return pl.pallas_call(
        kern,
        grid=(B, L // tr),
        in_specs=[main(0), main(1), main(2), prev(0), prev(1), prev(2), nxt(0), nxt(1), nxt(2),
                  full((CONV_K, W)), full((1, W)), full((1, W)), full(pw.shape), full((1, W))],
        out_specs=pl.BlockSpec((None, tr, 2 * W), lambda b, i: (b, i, 0)),
        out_shape=jax.ShapeDtypeStruct((B, L, 2 * W), BF16),
        scratch_shapes=[pltpu.VMEM((SUBLANES, tr + 2 * HALO, W), F32),
                        pltpu.VMEM((SUBLANES, tr + 2 * HALO, W), F32),
                        pltpu.VMEM((tr, W), F32)],
        compiler_params=_cparams(("parallel", "arbitrary")),
        name="conv_pool",
    )(z, z, z, z, z, z, z, z, z, dw, lnw.reshape(1, W), lnb.reshape(1, W), pw, ps.reshape(1, W))


def _split(x):
    hi = x.astype(BF16)
    lo = (x - hi.astype(F32)).astype(BF16)
    return hi, lo


def _peer_q_kernel(x_ref, nw_ref, mod_ref, wh_ref, h_ref, q_ref, hh_ref,
                   *, tm, n_lat_rows, ks, ctx_row):
    b, i, j = pl.program_id(0), pl.program_id(1), pl.program_id(2)

    @pl.when(j == 0)
    def _():
        is_ctx = _is_ctx_col(i, tm, n_lat_rows)
        shift = _mod_rows(mod_ref, b, ks[0], is_ctx, ctx_row)
        scale = _mod_rows(mod_ref, b, ks[1], is_ctx, ctx_row)
        hi = _norm_mod(x_ref[...], nw_ref[...], shift, scale).astype(BF16)
        hh_ref[...] = hi
        h_ref[...] = hi

    q_ref[...] = _dot(hh_ref[...], wh_ref[...])


def _peer_q(x, nw, mod, wh, *, ks, n_lat_rows, tm, tn):
    B, L, D = x.shape
    n = wh.shape[1]
    kern = functools.partial(_peer_q_kernel, tm=tm, n_lat_rows=n_lat_rows, ks=ks, ctx_row=B)
    return pl.pallas_call(
        kern,
        grid=(B, L // tm, n // tn),
        in_specs=[
            pl.BlockSpec((None, tm, D), lambda b, i, j: (b, i, 0)),
            pl.BlockSpec((1, D), lambda b, i, j: (0, 0)),
            pl.BlockSpec(mod.shape, lambda b, i, j: (0, 0, 0)),
            pl.BlockSpec((D, tn), lambda b, i, j: (0, j)),
        ],
        out_specs=[pl.BlockSpec((None, tm, D), lambda b, i, j: (b, i, 0)),
                   pl.BlockSpec((None, tm, tn), lambda b, i, j: (b, i, j))],
        out_shape=[jax.ShapeDtypeStruct((B, L, D), BF16), jax.ShapeDtypeStruct((B, L, n), F32)],
        scratch_shapes=[pltpu.VMEM((tm, D), BF16)],
        compiler_params=_cparams(("parallel", "parallel", "arbitrary")),
        name="peer_q",
    )(x, nw.reshape(1, D), mod, wh)


def _bitonic_merge_desc(v):
    n = len(v)
    j = n // 2
    while j >= 1:
        for i in range(n):
            l = i ^ j
            if l > i:
                a, b = v[i], v[l]
                v[i], v[l] = jnp.maximum(a, b), jnp.minimum(a, b)
        j //= 2
    return v


def _bitonic_sort_desc(v):
    n = len(v)
    k = 2
    while k <= n:
        j = k // 2
        while j >= 1:
            for i in range(n):
                l = i ^ j
                if l > i:
                    a, b = v[i], v[l]
                    hi, lo = jnp.maximum(a, b), jnp.minimum(a, b)
                    if (i & k) == 0:
                        v[i], v[l] = hi, lo
                    else:
                        v[i], v[l] = lo, hi
            j //= 2
        k *= 2
    return v


def _merge_top(x, y):
    n = len(x)
    c = [x[i] if y[n - 1 - i] is None else jnp.maximum(x[i], y[n - 1 - i]) for i in range(n)]
    return _bitonic_merge_desc(c)


def _peer_topk_kernel(q_ref, kh_ref, kl_ref, thr_ref, c1_ref, s2_ref, e2_ref, s1_scr, top_scr,
                      *, tt, n_heads, n_keys, topk):
    U = SUBLANES
    nv = n_keys // U
    sub = lax.broadcasted_iota(jnp.int32, (U, tt), 0)
    ap = [None] * topk
    bp = [None] * topk
    for h in range(n_heads):
        for p in range(2):
            c0 = (h * 2 + p) * n_keys
            qh, ql = _split(q_ref[:, c0:c0 + n_keys])
            kh, kl = kh_ref[h, p], kl_ref[h, p]
            st = _dot_nt(kh, qh) + (_dot_nt(kh, ql) + _dot_nt(kl, qh))
            if p == 0:
                s1_scr[h] = st
            else:
                s2_ref[h * n_keys:(h + 1) * n_keys, :] = st
            v = _bitonic_sort_desc([st[U * i:U * (i + 1), :] for i in range(nv)])
            for s in (4, 2, 1):
                v = _merge_top(v, [pltpu.roll(x, s, 0) for x in v])
            for a in range(topk):
                top_scr[h, p, a] = v[a]
                pk = ap if p == 0 else bp
                pk[a] = v[a] if h == 0 else jnp.where(sub == h, v[a], pk[a])
    acc = [ap[0] + bp[b] for b in range(topk)]
    for a in range(1, topk):
        na = topk // (a + 1)
        ya = [ap[a] + bp[b] if b < na else None for b in range(topk)]
        acc = _merge_top(acc, ya)
    v16p = acc[topk - 1]
    zp = jnp.zeros((U, tt), F32)
    for a in range(topk):
        zp = zp + jnp.exp(acc[a] - acc[0])
    rzp = 1.0 / zp
    for h in range(n_heads):
        v16 = jnp.broadcast_to(v16p[h:h + 1, :], (U, tt))
        rz = jnp.broadcast_to(rzp[h:h + 1, :], (U, tt))
        m1 = top_scr[h, 0, 0]
        m2 = top_scr[h, 1, 0]
        bs = [top_scr[h, 1, b] for b in range(topk)]
        for i in range(nv):
            rs = slice(h * n_keys + U * i, h * n_keys + U * (i + 1))
            x1 = s1_scr[h, U * i:U * (i + 1), :]
            thr = jnp.full((U, tt), jnp.inf, F32)
            for b in range(topk):
                thr = jnp.where(x1 + bs[b] >= v16, bs[b], thr)
            thr_ref[rs, :] = thr
            c1_ref[rs, :] = jnp.exp(x1 - m1) * rz
            e2_ref[rs, :] = jnp.exp(s2_ref[rs, :] - m2)


def _peer_topk(q, kh, kl, *, tt):
    B, L, _ = q.shape
    n_heads, _, n_keys, dk = kh.shape
    rows = n_heads * n_keys
    kern = functools.partial(_peer_topk_kernel, tt=tt, n_heads=n_heads, n_keys=n_keys, topk=PEER_TOPK)
    out = jax.ShapeDtypeStruct((B, rows, L), F32)
    ospec = pl.BlockSpec((None, rows, tt), lambda b, i: (b, 0, i))
    return pl.pallas_call(
        kern,
        grid=(B, L // tt),
        in_specs=[
            pl.BlockSpec((None, tt, q.shape[2]), lambda b, i: (b, i, 0)),
            pl.BlockSpec(kh.shape, lambda b, i: (0, 0, 0, 0)),
            pl.BlockSpec(kl.shape, lambda b, i: (0, 0, 0, 0)),
        ],
        out_specs=[ospec] * 4,
        out_shape=[out] * 4,
        scratch_shapes=[pltpu.VMEM((n_heads, n_keys, tt), F32),
                        pltpu.VMEM((n_heads, 2, PEER_TOPK, SUBLANES, tt), F32)],
        compiler_params=_cparams(("parallel", "arbitrary")),
        name="peer_topk",
    )(q, kh, kl)


PEER_RG = 16


def _peer_kernel(h_ref, u_ref, vt_ref, thr_ref, c1_ref, s2_ref, e2_ref, x_ref, mod_ref, o_ref,
                 acc_ref, st_ref, a_ref, *, tm, te, n_e, n_lat_rows, n_heads, n_keys, ctx_row):
    b, i, e = pl.program_id(0), pl.program_id(1), pl.program_id(2)

    @pl.when(e == 0)
    def _():
        acc_ref[...] = jnp.zeros_like(acc_ref)

    st_ref[...] = _dot_nt(u_ref[...], h_ref[...])
    n_i1 = te // n_keys
    for s in range(n_i1):
        i1 = e * n_i1 + s
        thr_rows = [thr_ref[pl.ds(hh * n_keys + i1, 1), :] for hh in range(n_heads)]
        c1_rows = [c1_ref[pl.ds(hh * n_keys + i1, 1), :] for hh in range(n_heads)]

        def rows(r, carry):
            r0 = pl.multiple_of(r * PEER_RG, PEER_RG)
            w = jnp.zeros((PEER_RG, tm), F32)
            for hh in range(n_heads):
                s2 = s2_ref[pl.ds(hh * n_keys + r0, PEER_RG), :]
                e2 = e2_ref[pl.ds(hh * n_keys + r0, PEER_RG), :]
                w = w + jnp.where(s2 >= thr_rows[hh], e2, 0.0) * c1_rows[hh]
            sv = st_ref[pl.ds(s * n_keys + r0, PEER_RG), :]
            g = 0.5 * sv * (1.0 + lax.erf(sv * INV_SQRT2))
            a_ref[pl.ds(s * n_keys + r0, PEER_RG), :] = (g * w).astype(BF16)
            return carry

        lax.fori_loop(0, n_keys // PEER_RG, rows, 0)

    acc_ref[...] += _dot(vt_ref[...], a_ref[...])

    @pl.when(e == n_e - 1)
    def _():
        is_ctx = _is_ctx_col(i, tm, n_lat_rows)
        gate = jnp.where(is_ctx, mod_ref[ctx_row], mod_ref[b])
        o_ref[...] = x_ref[...] + gate * acc_ref[...].T


def _peer_main(h, u, vt, thr, c1, s2, e2, x, mod_gate, *, n_lat_rows, tm, te):
    B, L, D = x.shape
    E = u.shape[0]
    rows = thr.shape[1]
    n_e = E // te
    kern = functools.partial(_peer_kernel, tm=tm, te=te, n_e=n_e, n_lat_rows=n_lat_rows,
                             n_heads=PEER_HEADS, n_keys=PEER_KEYS, ctx_row=B)
    tok = pl.BlockSpec((None, rows, tm), lambda b, i, e: (b, 0, i))
    return pl.pallas_call(
        kern,
        grid=(B, L // tm, n_e),
        in_specs=[
            pl.BlockSpec((None, tm, D), lambda b, i, e: (b, i, 0)),
            pl.BlockSpec((te, D), lambda b, i, e: (e, 0)),
            pl.BlockSpec((D, te), lambda b, i, e: (0, e)),
            tok, tok, tok, tok,
            pl.BlockSpec((None, tm, D), lambda b, i, e: (b, i, 0)),
            pl.BlockSpec(mod_gate.shape, lambda b, i, e: (0, 0, 0)),
        ],
        out_specs=pl.BlockSpec((None, tm, D), lambda b, i, e: (b, i, 0)),
        out_shape=jax.ShapeDtypeStruct((B, L, D), F32),
        scratch_shapes=[pltpu.VMEM((D, tm), F32), pltpu.VMEM((te, tm), F32), pltpu.VMEM((te, tm), BF16)],
        compiler_params=_cparams(("parallel", "parallel", "arbitrary")),
        name="peer_main",
    )(h, u, vt, thr, c1, s2, e2, x, mod_gate)


def _final_norm_kernel(x_ref, w_ref, o_ref):
    x = x_ref[...]
    ms = jnp.mean(x * x, axis=-1, keepdims=True)
    o_ref[...] = x * lax.rsqrt(ms + EPS) * w_ref[...]


def _final_norm(x, w, *, n_lat, tr):
    B, _, D = x.shape
    return pl.pallas_call(
        _final_norm_kernel,
        grid=(B, n_lat // tr),
        in_specs=[pl.BlockSpec((None, tr, D), lambda b, i: (b, i, 0)),
                  pl.BlockSpec((1, D), lambda b, i: (0, 0))],
        out_specs=pl.BlockSpec((None, tr, D), lambda b, i: (b, i, 0)),
        out_shape=jax.ShapeDtypeStruct((B, n_lat, D), F32),
        compiler_params=_cparams(("parallel", "arbitrary")),
        name="final_norm",
    )(x, w.reshape(1, D))


TILE_ROWS = 256
TILE_MM = 768
TILE_N = 512
TILE_PEER_TOK = 384
TILE_PEER_EXP = 512
TILE_TOPK_TOK = 256
TILE_MOD_N = 1024


def kernel(x, c, ctx, c_ctx, ada_w, ada_b, norm1_w, norm2_w, final_norm_w, ev_in_w, ev_out_w,
           hg_lb_logits, hg_norm_w, od_in_w, od_out_w, conv_dw, conv_ln_w, conv_ln_b, pool_w,
           pool_scale, peer_q_w, peer_keys, peer_u, peer_v):
    B, S, D = x.shape
    n_ctx = ctx.shape[1]
    depth = ada_w.shape[0]
    fw = ev_out_w.shape[1] // 2
    assert n_ctx == TILE_ROWS and S % TILE_ROWS == 0

    pos = _sincos_table(S // GRID_W, D)
    X = _embed(x, pos, ctx, TILE_ROWS)

    rows = SUBLANES
    cvec = jnp.zeros((rows, D), F32).at[:B].set(c).at[B].set(c_ctx)
    mods = _modulation(cvec, ada_w, ada_b, TILE_MOD_N)

    lb_p = jax.nn.softmax(hg_lb_logits.astype(F32), axis=0)
    lower = jnp.concatenate([jnp.zeros_like(lb_p[:1]), jnp.cumsum(lb_p[1:], axis=0)], axis=0)
    lower = jnp.clip(lower, 0.0, 1.0)

    mm = dict(n_lat_rows=S, tm=TILE_MM, tn=TILE_N)
    for l in range(depth):
        mod = mods[l].reshape(rows, N_MOD, D)
        j = l // 2
        if l % 2 == 0:
            w_in = ev_in_w[j].astype(BF16)
            a = _norm_matmul(X, norm1_w[l], mod, w_in[:, :fw], ks=(0, 1), **mm)
            z = _norm_matmul(X, norm1_w[l], mod, w_in[:, fw:], ks=(0, 1), **mm)
            four = _fourier_mix(a, S, n_ctx)
            of = _hgrn(z, lower[j], reverse=False, n_lat=S, n_ctx=n_ctx, tc=TILE_ROWS, col0=0)
            ob = _hgrn(z, lower[j], reverse=True, n_lat=S, n_ctx=n_ctx, tc=TILE_ROWS, col0=0)
            mix = _hg_combine(of, ob, z, four, hg_norm_w[j], gcol=4, tm=TILE_ROWS)
            w_out = ev_out_w[j].astype(BF16)
        else:
            z = _norm_matmul(X, norm1_w[l], mod, od_in_w[j].astype(BF16), ks=(0, 1), **mm)
            mix = _conv_pool(z, conv_dw[j], conv_ln_w[j], conv_ln_b[j], pool_w[j].astype(BF16),
                             pool_scale[j], n_lat=S, n_ctx=n_ctx, tr=TILE_ROWS)
            w_out = od_out_w[j].astype(BF16)
        X = _out_proj(mix, w_out, X, mod[:, 2:3, :], **mm)

        h, q = _peer_q(X, norm2_w[l], mod, peer_q_w[l].astype(BF16), ks=(3, 4), n_lat_rows=S,
                       tm=TILE_PEER_TOK, tn=TILE_N)
        kh, kl = _split(peer_keys[l])
        thr, c1, s2, e2 = _peer_topk(q, kh, kl, tt=TILE_TOPK_TOK)
        X = _peer_main(h, peer_u[l].astype(BF16), peer_v[l].astype(BF16).T, thr, c1, s2, e2, X,
                       mod[:, 5:6, :], n_lat_rows=S, tm=TILE_PEER_TOK, te=TILE_PEER_EXP)

    return _final_norm(X, final_norm_w, n_lat=S, tr=TILE_ROWS)
```
